```python
import jax, jax.numpy as jnp
from jax import lax
import numpy as np

D_MODEL = 2048
BATCH = 2
SEQ = 16384
DEPTH = 4

N_MIXERS = 2
N_HEADS = 8
HEAD_DIM = D_MODEL // N_HEADS
D_FF = 2 * D_MODEL
POOL_WINDOWS = (2, 4, 8, 16)
N_POOL_GROUPS = len(POOL_WINDOWS)
POOL_GROUP = D_MODEL // N_POOL_GROUPS
Q_BLOCK = 128
EPS = 1e-6
N_ATT_LAYERS = (DEPTH + N_MIXERS - 1) // N_MIXERS
N_POOL_LAYERS = DEPTH // N_MIXERS
HI = lax.Precision.HIGHEST

kernel_name = "stick_breaking_pool_macaron_hybrid"


def rms_norm(x, g):
    xf = x.astype(jnp.float32)
    y = xf * lax.rsqrt(jnp.mean(xf * xf, axis=-1, keepdims=True) + EPS)
    return (y * g.astype(jnp.float32)).astype(x.dtype)


def swiglu(h, w_in, w_out):
    g, u = jnp.split(h @ w_in, 2, axis=-1)
    return (jax.nn.silu(g) * u) @ w_out


def strict_upper(n):
    r = jnp.arange(n, dtype=jnp.int32)
    return (r[:, None] > r[None, :]).astype(jnp.float32)


def stick_breaking_attention(h, w_qkv, w_o, qk_gains):
    B, S, _ = h.shape
    qkv = (h @ w_qkv).reshape(B, S, 3, N_HEADS, HEAD_DIM)
    q = rms_norm(qkv[:, :, 0], qk_gains[0]).transpose(0, 2, 1, 3).astype(jnp.float32)
    k = rms_norm(qkv[:, :, 1], qk_gains[1]).transpose(0, 2, 1, 3).astype(jnp.float32)
    v = qkv[:, :, 2].transpose(0, 2, 1, 3)
    scale = HEAD_DIM ** -0.5
    tri_in = strict_upper(Q_BLOCK)
    offs = jnp.arange(Q_BLOCK, dtype=jnp.int32)
    n_blocks = S // Q_BLOCK
    outs = []
    for i in range(n_blocks):
        nk = i + 1
        kl = nk * Q_BLOCK
        qb = q[:, :, i * Q_BLOCK:(i + 1) * Q_BLOCK]
        z = jnp.einsum('bhqd,bhkd->bhqk', qb, k[:, :, :kl]) * scale
        q_pos = i * Q_BLOCK + offs
        causal = jnp.arange(kl, dtype=jnp.int32)[None, :] < q_pos[:, None]
        log_1m = jnp.where(causal, jax.nn.log_sigmoid(-z), 0.0)
        lb = log_1m.reshape(B, N_HEADS, Q_BLOCK, nk, Q_BLOCK)
        rest_in = jnp.einsum('bhqnj,js->bhqns', lb, tri_in, precision=HI)
        rest_blk = jnp.einsum('bhqm,mn->bhqn', lb.sum(-1), strict_upper(nk), precision=HI)
        rest = (rest_in + rest_blk[..., None]).reshape(B, N_HEADS, Q_BLOCK, kl)
        a = jnp.where(causal, jnp.exp(jax.nn.log_sigmoid(z) + rest), 0.0)
        outs.append(jnp.einsum('bhqk,bhkd->bhqd', a.astype(v.dtype), v[:, :, :kl]))
    o = jnp.concatenate(outs, axis=2)
    o = o.transpose(0, 2, 1, 3).reshape(B, S, D_MODEL)
    return o @ w_o


def causal_cumsum(u):
    B, S, C = u.shape
    nb = S // Q_BLOCK
    ub = u.reshape(B, nb, Q_BLOCK, C)
    lower_incl = strict_upper(Q_BLOCK) + jnp.eye(Q_BLOCK, dtype=jnp.float32)
    within = jnp.einsum('ts,bnsc->bntc', lower_incl, ub, precision=HI)
    prev = jnp.einsum('nm,bmc->bnc', strict_upper(nb), ub.sum(2), precision=HI)
    return (within + prev[:, :, None]).reshape(B, S, C)


def multiscale_pool(h, w_pool, scale):
    B, S, _ = h.shape
    uf = h.astype(jnp.float32)
    cs = causal_cumsum(uf).reshape(B, S, N_POOL_GROUPS, POOL_GROUP)
    u = uf.reshape(B, S, N_POOL_GROUPS, POOL_GROUP)
    pos = jnp.arange(1, S + 1, dtype=jnp.float32)
    outs = []
    for g, w in enumerate(POOL_WINDOWS):
        cs_g = cs[:, :, g]
        lo = jnp.pad(cs_g[:, :S - w], ((0, 0), (w, 0), (0, 0)))
        cnt = jnp.minimum(pos, float(w))[None, :, None]
        p = (cs_g - lo) / cnt - u[:, :, g]
        outs.append(jnp.einsum('bsc,cd->bsd', p.astype(h.dtype), w_pool[g]))
    return jnp.concatenate(outs, axis=-1) * scale


def setup_inputs(seed: int = 0) -> dict:
    key = jax.random.key(seed)
    ks = jax.random.split(key, 12)
    f32 = jnp.float32
    x = jax.random.normal(ks[0], (BATCH, SEQ, D_MODEL), f32)
    norm_gains = 1.0 + 0.02 * jax.random.normal(ks[1], (DEPTH, 3, D_MODEL), f32)
    ffn1_w_in = jax.random.normal(ks[2], (DEPTH, D_MODEL, 2 * D_FF), f32) * D_MODEL ** -0.5
    ffn1_w_out = jax.random.normal(ks[3], (DEPTH, D_FF, D_MODEL), f32) * D_FF ** -0.5
    ffn2_w_in = jax.random.normal(ks[4], (DEPTH, D_MODEL, 2 * D_FF), f32) * D_MODEL ** -0.5
    ffn2_w_out = jax.random.normal(ks[5], (DEPTH, D_FF, D_MODEL), f32) * D_FF ** -0.5
    att_w_qkv = jax.random.normal(ks[6], (N_ATT_LAYERS, D_MODEL, 3 * D_MODEL), f32) * D_MODEL ** -0.5
    att_w_o = jax.random.normal(ks[7], (N_ATT_LAYERS, D_MODEL, D_MODEL), f32) * D_MODEL ** -0.5
    att_qk_gains = 1.0 + 0.02 * jax.random.normal(ks[8], (N_ATT_LAYERS, 2, HEAD_DIM), f32)
    pool_w = jax.random.normal(ks[9], (N_POOL_LAYERS, N_POOL_GROUPS, POOL_GROUP, POOL_GROUP), f32) * POOL_GROUP ** -0.5
    pool_scale = 1.0 + 0.1 * jax.random.normal(ks[10], (N_POOL_LAYERS, D_MODEL), f32)
    return {"x": x, "norm_gains": norm_gains,
            "ffn1_w_in": ffn1_w_in, "ffn1_w_out": ffn1_w_out,
            "ffn2_w_in": ffn2_w_in, "ffn2_w_out": ffn2_w_out,
            "att_w_qkv": att_w_qkv, "att_w_o": att_w_o, "att_qk_gains": att_qk_gains,
            "pool_w": pool_w, "pool_scale": pool_scale}


def reference(x, norm_gains, ffn1_w_in, ffn1_w_out, ffn2_w_in, ffn2_w_out,
              att_w_qkv, att_w_o, att_qk_gains, pool_w, pool_scale):
    for i in range(DEPTH):
        x = x + 0.5 * swiglu(rms_norm(x, norm_gains[i, 0]), ffn1_w_in[i], ffn1_w_out[i])
        h = rms_norm(x, norm_gains[i, 1])
        j = i // N_MIXERS
        if i % N_MIXERS == 0:
            x = x + stick_breaking_attention(h, att_w_qkv[j], att_w_o[j], att_qk_gains[j])
        else:
            x = x + multiscale_pool(h, pool_w[j], pool_scale[j])
        x = x + 0.5 * swiglu(rms_norm(x, norm_gains[i, 2]), ffn2_w_in[i], ffn2_w_out[i])
    return x
```

```python
import functools

import jax
import jax.numpy as jnp
from jax import lax
from jax.experimental import pallas as pl
from jax.experimental.pallas import tpu as pltpu

D_MODEL = 2048
N_HEADS = 8
HEAD_DIM = D_MODEL // N_HEADS
D_FF = 2 * D_MODEL
POOL_WINDOWS = (2, 4, 8, 16)
POOL_GROUP = D_MODEL // len(POOL_WINDOWS)
EPS = 1e-6
N_MIXERS = 2

F32 = jnp.float32
BF16 = jnp.bfloat16

VMEM_LIMIT_BYTES = 56 * 1024 * 1024

FFN_ROWS = 1024
FFN_COLS = 512
FFN_ROW_CHUNK = 256
QKV_ROWS = 1024
QKV_COLS = 512
ATT_Q = 256
ATT_K = 256
PROJ_ROWS = 512
POOL_ROWS = 256
POOL_HALO = 32


def _rms_norm(x, gain):
    ms = jnp.mean(x * x, axis=-1, keepdims=True)
    return x * lax.rsqrt(ms + EPS) * gain


def _params(*semantics):
    return pltpu.CompilerParams(dimension_semantics=semantics,
                                vmem_limit_bytes=VMEM_LIMIT_BYTES)


def _ffn_kernel(x_ref, gain_ref, wg_ref, wu_ref, wo_ref, o_ref, h_ref):
    j = pl.program_id(1)

    @pl.when(j == 0)
    def _():
        for r in range(FFN_ROWS // FFN_ROW_CHUNK):
            rows = pl.ds(r * FFN_ROW_CHUNK, FFN_ROW_CHUNK)
            x = x_ref[rows, :]
            h_ref[rows, :] = _rms_norm(x, gain_ref[...]).astype(BF16)
            o_ref[rows, :] = x

    for r in range(FFN_ROWS // FFN_ROW_CHUNK):
        rows = pl.ds(r * FFN_ROW_CHUNK, FFN_ROW_CHUNK)
        h = h_ref[rows, :]
        g = jnp.dot(h, wg_ref[...], preferred_element_type=F32)
        u = jnp.dot(h, wu_ref[...], preferred_element_type=F32)
        act = (g * jax.nn.sigmoid(g) * u).astype(BF16)
        o_ref[rows, :] += 0.5 * jnp.dot(act, wo_ref[...], preferred_element_type=F32)


def _ffn(x, gain, w_in, w_out):
    t = x.shape[0]
    n_ff = D_FF // FFN_COLS
    return pl.pallas_call(
        _ffn_kernel,
        grid=(t // FFN_ROWS, n_ff),
        in_specs=[
            pl.BlockSpec((FFN_ROWS, D_MODEL), lambda i, j: (i, 0)),
            pl.BlockSpec((1, D_MODEL), lambda i, j: (0, 0)),
            pl.BlockSpec((D_MODEL, FFN_COLS), lambda i, j: (0, j)),
            pl.BlockSpec((D_MODEL, FFN_COLS), lambda i, j: (0, j + n_ff)),
            pl.BlockSpec((FFN_COLS, D_MODEL), lambda i, j: (j, 0)),
        ],
        out_specs=pl.BlockSpec((FFN_ROWS, D_MODEL), lambda i, j: (i, 0)),
        out_shape=jax.ShapeDtypeStruct((t, D_MODEL), F32),
        scratch_shapes=[pltpu.VMEM((FFN_ROWS, D_MODEL), BF16)],
        compiler_params=_params("parallel", "arbitrary"),
        name="ffn",
    )(x, gain, w_in, w_in, w_out)


def _qkv_kernel(x_ref, gain_ref, w_ref, qk_gain_ref, o_ref, h_ref):
    j = pl.program_id(1)
    tiles_per_part = D_MODEL // QKV_COLS

    @pl.when(j == 0)
    def _():
        for r in range(QKV_ROWS // FFN_ROW_CHUNK):
            rows = pl.ds(r * FFN_ROW_CHUNK, FFN_ROW_CHUNK)
            h_ref[rows, :] = _rms_norm(x_ref[rows, :], gain_ref[...]).astype(BF16)

    y = jnp.dot(h_ref[...], w_ref[...], preferred_element_type=F32)

    @pl.when(j < 2 * tiles_per_part)
    def _():
        is_q = j < tiles_per_part
        gain = jnp.where(is_q, qk_gain_ref[0:1, :], qk_gain_ref[1:2, :])
        scale = jnp.where(is_q, HEAD_DIM ** -0.5, 1.0).astype(F32)
        for hh in range(QKV_COLS // HEAD_DIM):
            cols = slice(hh * HEAD_DIM, (hh + 1) * HEAD_DIM)
            o_ref[:, cols] = (_rms_norm(y[:, cols], gain) * scale).astype(BF16)

    @pl.when(j >= 2 * tiles_per_part)
    def _():
        o_ref[...] = y.astype(BF16)


def _qkv(x, gain, w_qkv, qk_gains):
    t = x.shape[0]
    return pl.pallas_call(
        _qkv_kernel,
        grid=(t // QKV_ROWS, 3 * D_MODEL // QKV_COLS),
        in_specs=[
            pl.BlockSpec((QKV_ROWS, D_MODEL), lambda i, j: (i, 0)),
            pl.BlockSpec((1, D_MODEL), lambda i, j: (0, 0)),
            pl.BlockSpec((D_MODEL, QKV_COLS), lambda i, j: (0, j)),
            pl.BlockSpec((2, HEAD_DIM), lambda i, j: (0, 0)),
        ],
        out_specs=pl.BlockSpec((QKV_ROWS, QKV_COLS), lambda i, j: (i, j)),
        out_shape=jax.ShapeDtypeStruct((t, 3 * D_MODEL), BF16),
        scratch_shapes=[pltpu.VMEM((QKV_ROWS, D_MODEL), BF16)],
        compiler_params=_params("parallel", "arbitrary"),
        name="qkv",
    )(x, gain, w_qkv, qk_gains)


def _attn_tile(q, k, v, tri, rest_right, masked):
    z = lax.dot_general(q, k, (((1,), (1,)), ((), ())), preferred_element_type=F32)
    soft = jnp.log(1.0 + jnp.exp(-jnp.abs(z)))
    log_b = jnp.minimum(z, 0.0) - soft
    log_1mb = -(jnp.maximum(z, 0.0) + soft)
    if masked:
        t_idx = lax.broadcasted_iota(jnp.int32, z.shape, 0)
        s_idx = lax.broadcasted_iota(jnp.int32, z.shape, 1)
        causal = s_idx < t_idx
        log_1mb = jnp.where(causal, log_1mb, 0.0)
    hi = log_1mb.astype(BF16)
    lo = (log_1mb - hi.astype(F32)).astype(BF16)
    rest_in = (jnp.dot(hi, tri, preferred_element_type=F32)
               + jnp.dot(lo, tri, preferred_element_type=F32))
    a = jnp.exp(log_b + rest_in + rest_right)
    if masked:
        a = jnp.where(causal, a, 0.0)
    pv = jnp.dot(a.astype(BF16), v, preferred_element_type=F32)
    tile_total = rest_in[:, 0:1] + log_1mb[:, 0:1]
    return pv, rest_right + tile_total


def _attn_kernel(q_ref, k_ref, v_ref, o_ref, acc_ref, rest_ref):
    i = pl.program_id(1)
    q = q_ref[...]
    j_idx = lax.broadcasted_iota(jnp.int32, (ATT_K, ATT_K), 0)
    s_idx = lax.broadcasted_iota(jnp.int32, (ATT_K, ATT_K), 1)
    tri = jnp.where(j_idx > s_idx, 1.0, 0.0).astype(BF16)

    def kv(kb):
        rows = pl.ds(pl.multiple_of(kb * ATT_K, ATT_K), ATT_K)
        return k_ref[rows, :], v_ref[rows, :]

    k, v = kv(i)
    pv, rest = _attn_tile(q, k, v, tri, jnp.zeros((ATT_Q, 1), F32), masked=True)
    acc_ref[...] = pv
    rest_ref[...] = rest

    def body(n, carry):
        k, v = kv(i - 1 - n)
        pv, rest = _attn_tile(q, k, v, tri, rest_ref[...], masked=False)
        acc_ref[...] += pv
        rest_ref[...] = rest
        return carry

    lax.fori_loop(0, i, body, 0)
    o_ref[...] = acc_ref[...].astype(BF16)


def _attention(qkv, batch, seq):
    nq = seq // ATT_Q
    return pl.pallas_call(
        _attn_kernel,
        grid=(batch * N_HEADS, nq),
        in_specs=[
            pl.BlockSpec((ATT_Q, HEAD_DIM), lambda bh, i: ((bh // N_HEADS) * nq + i, bh % N_HEADS)),
            pl.BlockSpec((seq, HEAD_DIM), lambda bh, i: (bh // N_HEADS, N_HEADS + bh % N_HEADS)),
            pl.BlockSpec((seq, HEAD_DIM), lambda bh, i: (bh // N_HEADS, 2 * N_HEADS + bh % N_HEADS)),
        ],
        out_specs=pl.BlockSpec((ATT_Q, HEAD_DIM), lambda bh, i: ((bh // N_HEADS) * nq + i, bh % N_HEADS)),
        out_shape=jax.ShapeDtypeStruct((batch * seq, D_MODEL), BF16),
        scratch_shapes=[pltpu.VMEM((ATT_Q, HEAD_DIM), F32), pltpu.VMEM((ATT_Q, 1), F32)],
        compiler_params=_params("parallel", "arbitrary"),
        name="stick_breaking_attention",
    )(qkv, qkv, qkv)


def _proj_kernel(x_ref, a_ref, w_ref, o_ref):
    o_ref[...] = x_ref[...] + jnp.dot(a_ref[...], w_ref[...], preferred_element_type=F32)


def _proj_residual(x, a, w):
    t = x.shape[0]
    return pl.pallas_call(
        _proj_kernel,
        grid=(t // PROJ_ROWS,),
        in_specs=[
            pl.BlockSpec((PROJ_ROWS, D_MODEL), lambda i: (i, 0)),
            pl.BlockSpec((PROJ_ROWS, D_MODEL), lambda i: (i, 0)),
            pl.BlockSpec((D_MODEL, D_MODEL), lambda i: (0, 0)),
        ],
        out_specs=pl.BlockSpec((PROJ_ROWS, D_MODEL), lambda i: (i, 0)),
        out_shape=jax.ShapeDtypeStruct((t, D_MODEL), F32),
        compiler_params=_params("parallel"),
        name="attn_out_proj",
    )(x, a, w)


def _pool_kernel(x_ref, halo_ref, gain_ref, w_ref, scale_ref, o_ref, *level_refs):
    i = pl.program_id(1)
    n_rows = POOL_ROWS + POOL_HALO
    gain = gain_ref[...]

    history = _rms_norm(halo_ref[0], gain)
    level_refs[0][0:POOL_HALO, :] = jnp.where(i > 0, history, 0.0)
    x = x_ref[0]
    level_refs[0][POOL_HALO:, :] = _rms_norm(x, gain)

    n_levels = len(POOL_WINDOWS)
    for lvl in range(1, n_levels + 1):
        shift = 2 ** (lvl - 1)
        start = 8 * lvl
        src, dst = level_refs[lvl - 1], level_refs[lvl]
        src_col0 = max(lvl - 2, 0) * POOL_GROUP
        dst_col0 = (lvl - 1) * POOL_GROUP
        cols = D_MODEL - dst_col0
        off = dst_col0 - src_col0
        rows = n_rows - start
        dst[start:, :] = (src[start:, off:off + cols]
                          + src[pl.ds(start - shift, rows), off:off + cols])

    pos = i * POOL_ROWS + lax.broadcasted_iota(jnp.int32, (POOL_ROWS, 1), 0) + 1
    for g, window in enumerate(POOL_WINDOWS):
        cols = slice(g * POOL_GROUP, (g + 1) * POOL_GROUP)
        wsum = level_refs[g + 1][POOL_HALO:, 0:POOL_GROUP]
        cnt = jnp.minimum(pos, window).astype(F32)
        p = wsum / cnt - level_refs[0][POOL_HALO:, cols]
        y = jnp.dot(p.astype(BF16), w_ref[g], preferred_element_type=F32)
        o_ref[0, :, cols] = x[:, cols] + y * scale_ref[:, cols]


def _pool(x3, gain, w_pool, scale):
    batch, seq, _ = x3.shape
    halo_per_tile = POOL_ROWS // POOL_HALO
    n_rows = POOL_ROWS + POOL_HALO
    n_levels = len(POOL_WINDOWS)
    level_scratch = [pltpu.VMEM((n_rows, D_MODEL), F32)] + [
        pltpu.VMEM((n_rows, D_MODEL - (lvl - 1) * POOL_GROUP), F32) for lvl in range(1, n_levels + 1)]
    return pl.pallas_call(
        _pool_kernel,
        grid=(batch, seq // POOL_ROWS),
        in_specs=[
            pl.BlockSpec((1, POOL_ROWS, D_MODEL), lambda b, i: (b, i, 0)),
            pl.BlockSpec((1, POOL_HALO, D_MODEL),
                         lambda b, i: (b, jnp.maximum(i * halo_per_tile - 1, 0), 0)),
            pl.BlockSpec((1, D_MODEL), lambda b, i: (0, 0)),
            pl.BlockSpec((n_levels, POOL_GROUP, POOL_GROUP), lambda b, i: (0, 0, 0)),
            pl.BlockSpec((1, D_MODEL), lambda b, i: (0, 0)),
        ],
        out_specs=pl.BlockSpec((1, POOL_ROWS, D_MODEL), lambda b, i: (b, i, 0)),
        out_shape=jax.ShapeDtypeStruct(x3.shape, F32),
        scratch_shapes=level_scratch,
        compiler_params=_params("parallel", "arbitrary"),
        name="multiscale_pool",
    )(x3, x3, gain, w_pool, scale)


def kernel(x, norm_gains, ffn1_w_in, ffn1_w_out, ffn2_w_in, ffn2_w_out,
           att_w_qkv, att_w_o, att_qk_gains, pool_w, pool_scale):
    batch, seq, d_model = x.shape
    depth = norm_gains.shape[0]
    xf = x.reshape(batch * seq, d_model)
    for i in range(depth):
        gains = norm_gains[i].reshape(3, 1, d_model)
        xf = _ffn(xf, gains[0], ffn1_w_in[i].astype(BF16), ffn1_w_out[i].astype(BF16))
        j = i // N_MIXERS
        if i % N_MIXERS == 0:
            qkv = _qkv(xf, gains[1], att_w_qkv[j].astype(BF16), att_qk_gains[j])
            a = _attention(qkv, batch, seq)
            xf = _proj_residual(xf, a, att_w_o[j].astype(BF16))
        else:
            x3 = _pool(xf.reshape(batch, seq, d_model), gains[1],
                       pool_w[j].astype(BF16), pool_scale[j].reshape(1, d_model))
            xf = x3.reshape(batch * seq, d_model)
        xf = _ffn(xf, gains[2], ffn2_w_in[i].astype(BF16), ffn2_w_out[i].astype(BF16))
    return xf.reshape(batch, seq, d_model)
```

```python
import jax
import jax.numpy as jnp
from jax import lax
from jax.experimental import pallas as pl
from jax.experimental.pallas import tpu as pltpu

D_MODEL = 2048
N_HEADS = 8
HEAD_DIM = D_MODEL // N_HEADS
D_FF = 2 * D_MODEL
POOL_WINDOWS = (2, 4, 8, 16)
POOL_GROUP = D_MODEL // len(POOL_WINDOWS)
EPS = 1e-6
N_MIXERS = 2
LOG2_E = 1.4426950408889634

F32 = jnp.float32
BF16 = jnp.bfloat16

SUBLANES = 8
VMEM_LIMIT_BYTES = 56 * 1024 * 1024

FFN_ROWS = 1024
FFN_COLS = 512
ROW_CHUNK = 256
QKV_ROWS = 1024
QKV_COLS = 512
ATT_T = 512
ATT_SEG = ATT_T // SUBLANES
ATT_LANES = 256
PROJ_ROWS = 512
POOL_ROWS = 256
POOL_HALO = 32


def _rms_norm(x, gain):
    ms = jnp.mean(x * x, axis=-1, keepdims=True)
    return x * lax.rsqrt(ms + EPS) * gain


def _params(*semantics):
    return pltpu.CompilerParams(dimension_semantics=semantics,
                                vmem_limit_bytes=VMEM_LIMIT_BYTES)


def _ffn_kernel(x_ref, gain_ref, wg_ref, wu_ref, wo_ref, o_ref, h_ref):
    j = pl.program_id(1)

    @pl.when(j == 0)
    def _():
        for r in range(FFN_ROWS // ROW_CHUNK):
            rows = pl.ds(r * ROW_CHUNK, ROW_CHUNK)
            x = x_ref[rows, :]
            h_ref[rows, :] = _rms_norm(x, gain_ref[...]).astype(BF16)
            o_ref[rows, :] = x

    for r in range(FFN_ROWS // ROW_CHUNK):
        rows = pl.ds(r * ROW_CHUNK, ROW_CHUNK)
        h = h_ref[rows, :]
        g = jnp.dot(h, wg_ref[...], preferred_element_type=F32)
        u = jnp.dot(h, wu_ref[...], preferred_element_type=F32)
        act = (g * jax.nn.sigmoid(g) * u).astype(BF16)
        o_ref[rows, :] += 0.5 * jnp.dot(act, wo_ref[...], preferred_element_type=F32)


def _ffn(x, gain, w_in, w_out):
    t = x.shape[0]
    n_ff = D_FF // FFN_COLS
    return pl.pallas_call(
        _ffn_kernel,
        grid=(t // FFN_ROWS, n_ff),
        in_specs=[
            pl.BlockSpec((FFN_ROWS, D_MODEL), lambda i, j: (i, 0)),
            pl.BlockSpec((1, D_MODEL), lambda i, j: (0, 0)),
            pl.BlockSpec((D_MODEL, FFN_COLS), lambda i, j: (0, j)),
            pl.BlockSpec((D_MODEL, FFN_COLS), lambda i, j: (0, j + n_ff)),
            pl.BlockSpec((FFN_COLS, D_MODEL), lambda i, j: (j, 0)),
        ],
        out_specs=pl.BlockSpec((FFN_ROWS, D_MODEL), lambda i, j: (i, 0)),
        out_shape=jax.ShapeDtypeStruct((t, D_MODEL), F32),
        scratch_shapes=[pltpu.VMEM((FFN_ROWS, D_MODEL), BF16)],
        compiler_params=_params("parallel", "arbitrary"),
        name="ffn",
    )(x, gain, w_in, w_in, w_out)


def _qkv_kernel(x_ref, gain_ref, w_ref, qk_gain_ref, q_ref, kp_ref, vt_ref, h_ref, hp_ref):
    j = pl.program_id(1)
    tiles_per_part = D_MODEL // QKV_COLS
    gain = gain_ref[...]

    @pl.when(j == 0)
    def _():
        for r in range(QKV_ROWS // ROW_CHUNK):
            rows = pl.ds(r * ROW_CHUNK, ROW_CHUNK)
            h_ref[rows, :] = _rms_norm(x_ref[rows, :], gain).astype(BF16)
        p_idx = lax.broadcasted_iota(jnp.int32, (ATT_T, ATT_T), 0)
        tok_idx = lax.broadcasted_iota(jnp.int32, (ATT_T, ATT_T), 1)
        perm = jnp.where(tok_idx == (p_idx % SUBLANES) * ATT_SEG + p_idx // SUBLANES, 1.0, 0.0).astype(BF16)
        for t in range(QKV_ROWS // ATT_T):
            rows = pl.ds(t * ATT_T, ATT_T)
            for c in range(D_MODEL // QKV_COLS):
                cols = pl.ds(c * QKV_COLS, QKV_COLS)
                hp_ref[rows, cols] = jnp.dot(perm, h_ref[rows, cols],
                                             preferred_element_type=F32).astype(BF16)

    def head_norm(y, which):
        out = []
        for hh in range(QKV_COLS // HEAD_DIM):
            cols = slice(hh * HEAD_DIM, (hh + 1) * HEAD_DIM)
            out.append(_rms_norm(y[:, cols], qk_gain_ref[which:which + 1, :]))
        return jnp.concatenate(out, axis=1)

    @pl.when(j < tiles_per_part)
    def _():
        y = jnp.dot(h_ref[...], w_ref[...], preferred_element_type=F32)
        q_ref[...] = (head_norm(y, 0) * HEAD_DIM ** -0.5).astype(BF16)

    @pl.when((j >= tiles_per_part) & (j < 2 * tiles_per_part))
    def _():
        y = jnp.dot(hp_ref[...], w_ref[...], preferred_element_type=F32)
        kp_ref[...] = head_norm(y, 1).astype(BF16)

    @pl.when(j >= 2 * tiles_per_part)
    def _():
        for t in range(QKV_ROWS // ATT_T):
            y = jnp.dot(hp_ref[pl.ds(t * ATT_T, ATT_T), :], w_ref[...], preferred_element_type=F32)
            vt_ref[t] = y.T.astype(BF16)


def _qkv(x, gain, w_qkv, qk_gains):
    t = x.shape[0]
    n_part = D_MODEL // QKV_COLS
    tiles_per_step = QKV_ROWS // ATT_T
    return pl.pallas_call(
        _qkv_kernel,
        grid=(t // QKV_ROWS, 3 * n_part),
        in_specs=[
            pl.BlockSpec((QKV_ROWS, D_MODEL), lambda i, j: (i, 0)),
            pl.BlockSpec((1, D_MODEL), lambda i, j: (0, 0)),
            pl.BlockSpec((D_MODEL, QKV_COLS), lambda i, j: (0, j)),
            pl.BlockSpec((2, HEAD_DIM), lambda i, j: (0, 0)),
        ],
        out_specs=[
            pl.BlockSpec((QKV_ROWS, QKV_COLS), lambda i, j: (i, jnp.minimum(j, n_part - 1))),
            pl.BlockSpec((QKV_ROWS, QKV_COLS), lambda i, j: (i, jnp.clip(j - n_part, 0, n_part - 1))),
            pl.BlockSpec((tiles_per_step, QKV_COLS, ATT_T),
                         lambda i, j: (i, jnp.clip(j - 2 * n_part, 0, n_part - 1), 0)),
        ],
        out_shape=[
            jax.ShapeDtypeStruct((t, D_MODEL), BF16),
            jax.ShapeDtypeStruct((t, D_MODEL), BF16),
            jax.ShapeDtypeStruct((t // ATT_T, D_MODEL, ATT_T), BF16),
        ],
        scratch_shapes=[pltpu.VMEM((QKV_ROWS, D_MODEL), BF16), pltpu.VMEM((QKV_ROWS, D_MODEL), BF16)],
        compiler_params=_params("arbitrary", "arbitrary"),
        name="qkv",
    )(x, gain, w_qkv, qk_gains)


def _shift_up_sublanes(x, k, sublane):
    return jnp.where(sublane < SUBLANES - k, pltpu.roll(x, SUBLANES - k, 0), 1.0)


def _attn_chain(z_t, right, first_query, masked):
    shape = (SUBLANES, ATT_LANES)
    sublane = lax.broadcasted_iota(jnp.int32, shape, 0)
    if masked:
        query = first_query + lax.broadcasted_iota(jnp.int32, shape, 1)
        slack = query - sublane * ATT_SEG
    sig = [None] * ATT_SEG
    after = [None] * ATT_SEG
    run = jnp.ones(shape, F32)
    for r in reversed(range(ATT_SEG)):
        z = z_t[r * SUBLANES:(r + 1) * SUBLANES, :]
        s = 1.0 / (1.0 + jnp.exp2(z * (-LOG2_E)))
        c = 1.0 - s
        if masked:
            causal = slack > r
            s = jnp.where(causal, s, 0.0)
            c = jnp.where(causal, c, 1.0)
        sig[r] = s
        after[r] = run
        run = run * c
    later = _shift_up_sublanes(run, 1, sublane)
    for k in (1, 2, 4):
        later = later * _shift_up_sublanes(later, k, sublane)
    total = jnp.broadcast_to((later * run)[0:1, :], shape)
    scale = later * right
    a_t = jnp.concatenate([sig[r] * (after[r] * scale) for r in range(ATT_SEG)], axis=0)
    return a_t.astype(BF16), right * total


def _attn_kernel(q_ref, kp_ref, vt_ref, o_ref, acc_ref, right_ref):
    i = pl.program_id(1)
    q_t = q_ref[...].astype(F32).T.astype(BF16)
    n_chains = ATT_T // ATT_LANES

    def tile(kb, masked):
        k = kp_ref[pl.ds(pl.multiple_of(kb * ATT_T, ATT_T), ATT_T), :]
        v_t = vt_ref[kb]
        for c in range(n_chains):
            lanes = slice(c * ATT_LANES, (c + 1) * ATT_LANES)
            z_t = jnp.dot(k, q_t[:, lanes], preferred_element_type=F32)
            right = jnp.ones((SUBLANES, ATT_LANES), F32) if masked else right_ref[:, lanes]
            a_t, right = _attn_chain(z_t, right, c * ATT_LANES, masked)
            pv = jnp.dot(v_t, a_t, preferred_element_type=F32)
            if masked:
                acc_ref[:, lanes] = pv
            else:
                acc_ref[:, lanes] += pv
            right_ref[:, lanes] = right

    tile(i, masked=True)

    def body(n, carry):
        tile(i - 1 - n, masked=False)
        return carry

    lax.fori_loop(0, i, body, 0)
    o_ref[...] = acc_ref[...].T.astype(BF16)


def _attention(q, kp, vt, batch, seq):
    nq = seq // ATT_T
    return pl.pallas_call(
        _attn_kernel,
        grid=(batch * N_HEADS, nq),
        in_specs=[
            pl.BlockSpec((ATT_T, HEAD_DIM), lambda bh, i: ((bh // N_HEADS) * nq + i, bh % N_HEADS)),
            pl.BlockSpec((seq, HEAD_DIM), lambda bh, i: (bh // N_HEADS, bh % N_HEADS)),
            pl.BlockSpec((nq, HEAD_DIM, ATT_T), lambda bh, i: (bh // N_HEADS, bh % N_HEADS, 0)),
        ],
        out_specs=pl.BlockSpec((ATT_T, HEAD_DIM), lambda bh, i: ((bh // N_HEADS) * nq + i, bh % N_HEADS)),
        out_shape=jax.ShapeDtypeStruct((batch * seq, D_MODEL), BF16),
        scratch_shapes=[pltpu.VMEM((HEAD_DIM, ATT_T), F32), pltpu.VMEM((SUBLANES, ATT_T), F32)],
        compiler_params=_params("parallel", "arbitrary"),
        name="stick_breaking_attention",
    )(q, kp, vt)


def _proj_kernel(x_ref, a_ref, w_ref, o_ref):
    o_ref[...] = x_ref[...] + jnp.dot(a_ref[...], w_ref[...], preferred_element_type=F32)


def _proj_residual(x, a, w):
    t = x.shape[0]
    return pl.pallas_call(
        _proj_kernel,
        grid=(t // PROJ_ROWS,),
        in_specs=[
            pl.BlockSpec((PROJ_ROWS, D_MODEL), lambda i: (i, 0)),
            pl.BlockSpec((PROJ_ROWS, D_MODEL), lambda i: (i, 0)),
            pl.BlockSpec((D_MODEL, D_MODEL), lambda i: (0, 0)),
        ],
        out_specs=pl.BlockSpec((PROJ_ROWS, D_MODEL), lambda i: (i, 0)),
        out_shape=jax.ShapeDtypeStruct((t, D_MODEL), F32),
        compiler_params=_params("parallel"),
        name="attn_out_proj",
    )(x, a, w)


def _pool_kernel(x_ref, halo_ref, gain_ref, w_ref, scale_ref, o_ref, *level_refs):
    i = pl.program_id(1)
    n_rows = POOL_ROWS + POOL_HALO
    gain = gain_ref[...]

    history = _rms_norm(halo_ref[0], gain)
    level_refs[0][0:POOL_HALO, :] = jnp.where(i > 0, history, 0.0)
    x = x_ref[0]
    level_refs[0][POOL_HALO:, :] = _rms_norm(x, gain)

    n_levels = len(POOL_WINDOWS)
    for lvl in range(1, n_levels + 1):
        shift = 2 ** (lvl - 1)
        start = SUBLANES * lvl
        src, dst = level_refs[lvl - 1], level_refs[lvl]
        src_col0 = max(lvl - 2, 0) * POOL_GROUP
        dst_col0 = (lvl - 1) * POOL_GROUP
        cols = D_MODEL - dst_col0
        off = dst_col0 - src_col0
        rows = n_rows - start
        dst[start:, :] = (src[start:, off:off + cols]
                          + src[pl.ds(start - shift, rows), off:off + cols])

    pos = i * POOL_ROWS + lax.broadcasted_iota(jnp.int32, (POOL_ROWS, 1), 0) + 1
    for g, window in enumerate(POOL_WINDOWS):
        cols = slice(g * POOL_GROUP, (g + 1) * POOL_GROUP)
        wsum = level_refs[g + 1][POOL_HALO:, 0:POOL_GROUP]
        cnt = jnp.minimum(pos, window).astype(F32)
        p = wsum / cnt - level_refs[0][POOL_HALO:, cols]
        y = jnp.dot(p.astype(BF16), w_ref[g], preferred_element_type=F32)
        o_ref[0, :, cols] = x[:, cols] + y * scale_ref[:, cols]


def _pool(x3, gain, w_pool, scale):
    batch, seq, _ = x3.shape
    halo_per_tile = POOL_ROWS // POOL_HALO
    n_rows = POOL_ROWS + POOL_HALO
    n_levels = len(POOL_WINDOWS)
    level_scratch = [pltpu.VMEM((n_rows, D_MODEL), F32)] + [
        pltpu.VMEM((n_rows, D_MODEL - (lvl - 1) * POOL_GROUP), F32) for lvl in range(1, n_levels + 1)]
    return pl.pallas_call(
        _pool_kernel,
        grid=(batch, seq // POOL_ROWS),
        in_specs=[
            pl.BlockSpec((1, POOL_ROWS, D_MODEL), lambda b, i: (b, i, 0)),
            pl.BlockSpec((1, POOL_HALO, D_MODEL),
                         lambda b, i: (b, jnp.maximum(i * halo_per_tile - 1, 0), 0)),
            pl.BlockSpec((1, D_MODEL), lambda b, i: (0, 0)),
            pl.BlockSpec((n_levels, POOL_GROUP, POOL_GROUP), lambda b, i: (0, 0, 0)),
            pl.BlockSpec((1, D_MODEL), lambda b, i: (0, 0)),
        ],
        out_specs=pl.BlockSpec((1, POOL_ROWS, D_MODEL), lambda b, i: (b, i, 0)),
        out_shape=jax.ShapeDtypeStruct(x3.shape, F32),
        scratch_shapes=level_scratch,
        compiler_params=_params("parallel", "arbitrary"),
        name="multiscale_pool",
    )(x3, x3, gain, w_pool, scale)


def kernel(x, norm_gains, ffn1_w_in, ffn1_w_out, ffn2_w_in, ffn2_w_out,
           att_w_qkv, att_w_o, att_qk_gains, pool_w, pool_scale):
    batch, seq, d_model = x.shape
    depth = norm_gains.shape[0]
    xf = x.reshape(batch * seq, d_model)
    for i in range(depth):
        gains = norm_gains[i].reshape(3, 1, d_model)
        xf = _ffn(xf, gains[0], ffn1_w_in[i].astype(BF16), ffn1_w_out[i].astype(BF16))
        j = i // N_MIXERS
        if i % N_MIXERS == 0:
            q, kp, vt = _qkv(xf, gains[1], att_w_qkv[j].astype(BF16), att_qk_gains[j])
            a = _attention(q, kp, vt, batch, seq)
            xf = _proj_residual(xf, a, att_w_o[j].astype(BF16))
        else:
            x3 = _pool(xf.reshape(batch, seq, d_model), gains[1],
                       pool_w[j].astype(BF16), pool_scale[j].reshape(1, d_model))
            xf = x3.reshape(batch * seq, d_model)
        xf = _ffn(xf, gains[2], ffn2_w_in[i].astype(BF16), ffn2_w_out[i].astype(BF16))
    return xf.reshape(batch, seq, d_model)
```

```python
import jax
import jax.numpy as jnp
from jax import lax
from jax.experimental import pallas as pl
from jax.experimental.pallas import tpu as pltpu

D_MODEL = 2048
N_HEADS = 8
HEAD_DIM = D_MODEL // N_HEADS
D_FF = 2 * D_MODEL
POOL_WINDOWS = (2, 4, 8, 16)
POOL_GROUP = D_MODEL // len(POOL_WINDOWS)
EPS = 1e-6
N_MIXERS = 2
Q_SCALE = 0.5 * HEAD_DIM ** -0.5

F32 = jnp.float32
BF16 = jnp.bfloat16

SUBLANES = 8
VMEM_LIMIT_BYTES = 56 * 1024 * 1024

FFN_ROWS = 1024
FFN_COLS = 512
ROW_CHUNK = 256
QKV_ROWS = 1024
QKV_COLS = 512
ATT_T = 512
ATT_SEG = ATT_T // SUBLANES
ATT_LANES = 256
ATT_UNROLL = 4
PROJ_ROWS = 512
POOL_ROWS = 256
POOL_HALO = 32


def _rms_norm(x, gain):
    ms = jnp.mean(x * x, axis=-1, keepdims=True)
    return x * lax.rsqrt(ms + EPS) * gain


def _params(*semantics, flags=None):
    return pltpu.CompilerParams(dimension_semantics=semantics,
                                vmem_limit_bytes=VMEM_LIMIT_BYTES, flags=flags)


def _ffn_kernel(x_ref, gain_ref, wg_ref, wu_ref, wo_ref, o_ref, h_ref):
    j = pl.program_id(1)

    @pl.when(j == 0)
    def _():
        for r in range(FFN_ROWS // ROW_CHUNK):
            rows = pl.ds(r * ROW_CHUNK, ROW_CHUNK)
            x = x_ref[rows, :]
            h_ref[rows, :] = _rms_norm(x, gain_ref[...]).astype(BF16)
            o_ref[rows, :] = x

    for r in range(FFN_ROWS // ROW_CHUNK):
        rows = pl.ds(r * ROW_CHUNK, ROW_CHUNK)
        h = h_ref[rows, :]
        g = jnp.dot(h, wg_ref[...], preferred_element_type=F32)
        u = jnp.dot(h, wu_ref[...], preferred_element_type=F32)
        act = (g * jax.nn.sigmoid(g) * u).astype(BF16)
        o_ref[rows, :] += 0.5 * jnp.dot(act, wo_ref[...], preferred_element_type=F32)


def _ffn(x, gain, w_in, w_out):
    t = x.shape[0]
    n_ff = D_FF // FFN_COLS
    return pl.pallas_call(
        _ffn_kernel,
        grid=(t // FFN_ROWS, n_ff),
        in_specs=[
            pl.BlockSpec((FFN_ROWS, D_MODEL), lambda i, j: (i, 0)),
            pl.BlockSpec((1, D_MODEL), lambda i, j: (0, 0)),
            pl.BlockSpec((D_MODEL, FFN_COLS), lambda i, j: (0, j)),
            pl.BlockSpec((D_MODEL, FFN_COLS), lambda i, j: (0, j + n_ff)),
            pl.BlockSpec((FFN_COLS, D_MODEL), lambda i, j: (j, 0)),
        ],
        out_specs=pl.BlockSpec((FFN_ROWS, D_MODEL), lambda i, j: (i, 0)),
        out_shape=jax.ShapeDtypeStruct((t, D_MODEL), F32),
        scratch_shapes=[pltpu.VMEM((FFN_ROWS, D_MODEL), BF16)],
        compiler_params=_params("parallel", "arbitrary"),
        name="ffn",
    )(x, gain, w_in, w_in, w_out)


def _qkv_kernel(x_ref, gain_ref, w_ref, qk_gain_ref, q_ref, kp_ref, vt_ref, h_ref, hp_ref):
    j = pl.program_id(1)
    tiles_per_part = D_MODEL // QKV_COLS
    gain = gain_ref[...]

    @pl.when(j == 0)
    def _():
        for r in range(QKV_ROWS // ROW_CHUNK):
            rows = pl.ds(r * ROW_CHUNK, ROW_CHUNK)
            h_ref[rows, :] = _rms_norm(x_ref[rows, :], gain).astype(BF16)
        p_idx = lax.broadcasted_iota(jnp.int32, (ATT_T, ATT_T), 0)
        tok_idx = lax.broadcasted_iota(jnp.int32, (ATT_T, ATT_T), 1)
        perm = jnp.where(tok_idx == (p_idx % SUBLANES) * ATT_SEG + p_idx // SUBLANES, 1.0, 0.0).astype(BF16)
        for t in range(QKV_ROWS // ATT_T):
            rows = pl.ds(t * ATT_T, ATT_T)
            for c in range(D_MODEL // QKV_COLS):
                cols = pl.ds(c * QKV_COLS, QKV_COLS)
                hp_ref[rows, cols] = jnp.dot(perm, h_ref[rows, cols],
                                             preferred_element_type=F32).astype(BF16)

    def head_norm(y, which):
        out = []
        for hh in range(QKV_COLS // HEAD_DIM):
            cols = slice(hh * HEAD_DIM, (hh + 1) * HEAD_DIM)
            out.append(_rms_norm(y[:, cols], qk_gain_ref[which:which + 1, :]))
        return jnp.concatenate(out, axis=1)

    @pl.when(j < tiles_per_part)
    def _():
        y = jnp.dot(h_ref[...], w_ref[...], preferred_element_type=F32)
        q_ref[...] = (head_norm(y, 0) * Q_SCALE).astype(BF16)

    @pl.when((j >= tiles_per_part) & (j < 2 * tiles_per_part))
    def _():
        y = jnp.dot(hp_ref[...], w_ref[...], preferred_element_type=F32)
        kp_ref[...] = head_norm(y, 1).astype(BF16)

    @pl.when(j >= 2 * tiles_per_part)
    def _():
        for t in range(QKV_ROWS // ATT_T):
            y = jnp.dot(hp_ref[pl.ds(t * ATT_T, ATT_T), :], w_ref[...], preferred_element_type=F32)
            vt_ref[t] = y.T.astype(BF16)


def _qkv(x, gain, w_qkv, qk_gains):
    t = x.shape[0]
    n_part = D_MODEL // QKV_COLS
    tiles_per_step = QKV_ROWS // ATT_T
    return pl.pallas_call(
        _qkv_kernel,
        grid=(t // QKV_ROWS, 3 * n_part),
        in_specs=[
            pl.BlockSpec((QKV_ROWS, D_MODEL), lambda i, j: (i, 0)),
            pl.BlockSpec((1, D_MODEL), lambda i, j: (0, 0)),
            pl.BlockSpec((D_MODEL, QKV_COLS), lambda i, j: (0, j)),
            pl.BlockSpec((2, HEAD_DIM), lambda i, j: (0, 0)),
        ],
        out_specs=[
            pl.BlockSpec((QKV_ROWS, QKV_COLS), lambda i, j: (i, jnp.minimum(j, n_part - 1))),
            pl.BlockSpec((QKV_ROWS, QKV_COLS), lambda i, j: (i, jnp.clip(j - n_part, 0, n_part - 1))),
            pl.BlockSpec((tiles_per_step, QKV_COLS, ATT_T),
                         lambda i, j: (i, jnp.clip(j - 2 * n_part, 0, n_part - 1), 0)),
        ],
        out_shape=[
            jax.ShapeDtypeStruct((t, D_MODEL), BF16),
            jax.ShapeDtypeStruct((t, D_MODEL), BF16),
            jax.ShapeDtypeStruct((t // ATT_T, D_MODEL, ATT_T), BF16),
        ],
        scratch_shapes=[pltpu.VMEM((QKV_ROWS, D_MODEL), BF16), pltpu.VMEM((QKV_ROWS, D_MODEL), BF16)],
        compiler_params=_params("arbitrary", "arbitrary"),
        name="qkv",
    )(x, gain, w_qkv, qk_gains)


def _shift_up_sublanes(x, k, sublane):
    return jnp.where(sublane < SUBLANES - k, pltpu.roll(x, SUBLANES - k, 0), 1.0)


def _attn_chain(z_slab, right, first_query, masked, valid):
    shape = (SUBLANES, ATT_LANES)
    sublane = lax.broadcasted_iota(jnp.int32, shape, 0)
    if masked:
        query = first_query + lax.broadcasted_iota(jnp.int32, shape, 1)
        slack = query - sublane * ATT_SEG
    partial = [None] * ATT_SEG
    run = jnp.ones(shape, F32)
    for r in reversed(range(ATT_SEG)):
        half = 0.5 * jnp.tanh(z_slab(r))
        s = 0.5 + half
        c = 0.5 - half
        if masked:
            causal = slack > r
            s = jnp.where(causal, s, 0.0)
            c = jnp.where(causal, c, 1.0)
        partial[r] = s * run
        run = run * c
        if r % 2 == 0:
            partial[r] = jnp.concatenate([partial[r], partial[r + 1]], axis=0).astype(BF16)
    later = _shift_up_sublanes(run, 1, sublane)
    for k in (1, 2, 4):
        later = later * _shift_up_sublanes(later, k, sublane)
    total = jnp.broadcast_to((later * run)[0:1, :], shape)
    scale = later * right if valid is None else later * (right * valid)
    scale = jnp.concatenate([scale, scale], axis=0).astype(BF16)
    a_t = jnp.concatenate([partial[r] * scale for r in range(0, ATT_SEG, 2)], axis=0)
    return a_t, right * total


def _attn_kernel(q_ref, kp_ref, vt_ref, o_ref, acc_ref, right_ref, z0_ref, z1_ref, a0_ref, a1_ref):
    i = pl.program_id(1)
    q_t = q_ref[...].astype(F32).T.astype(BF16)
    z_refs = (z0_ref, z1_ref)
    a_refs = (a0_ref, a1_ref)

    def scores(kb):
        k = kp_ref[pl.ds(pl.multiple_of(kb * ATT_T, ATT_T), ATT_T), :]
        return jnp.dot(k, q_t, preferred_element_type=F32)

    def elementwise(slot, masked, valid):
        z_ref, a_ref = z_refs[slot], a_refs[slot]
        for c in range(ATT_T // ATT_LANES):
            lanes = pl.ds(c * ATT_LANES, ATT_LANES)
            right = jnp.ones((SUBLANES, ATT_LANES), F32) if masked else right_ref[:, lanes]
            a_t, right = _attn_chain(lambda r: z_ref[pl.ds(r * SUBLANES, SUBLANES), lanes],
                                     right, c * ATT_LANES, masked, valid)
            a_ref[:, lanes] = a_t
            right_ref[:, lanes] = right

    def values(kb, slot):
        return jnp.dot(vt_ref[kb], a_refs[slot][...], preferred_element_type=F32)

    def step(n, slot):
        kb = i - n
        z_refs[1 - slot][...] = scores(jnp.maximum(kb - 1, 0))
        acc_ref[...] += values(jnp.maximum(kb + 1, 0), 1 - slot)
        elementwise(slot, masked=False, valid=jnp.where(kb >= 0, 1.0, 0.0).astype(F32))

    z0_ref[...] = scores(i)
    z1_ref[...] = scores(jnp.maximum(i - 1, 0))
    elementwise(0, masked=True, valid=None)
    acc_ref[...] = jnp.zeros_like(acc_ref)

    def unrolled_steps(m, carry):
        for k in range(1, ATT_UNROLL + 1):
            step(ATT_UNROLL * m + k, k % 2)
        return carry

    lax.fori_loop(0, pl.cdiv(i, ATT_UNROLL), unrolled_steps, 0)
    o_ref[...] = (acc_ref[...] + values(0, 0)).T.astype(BF16)


def _attention(q, kp, vt, batch, seq):
    nq = seq // ATT_T
    return pl.pallas_call(
        _attn_kernel,
        grid=(batch * N_HEADS, nq),
        in_specs=[
            pl.BlockSpec((ATT_T, HEAD_DIM), lambda bh, i: ((bh // N_HEADS) * nq + i, bh % N_HEADS)),
            pl.BlockSpec((seq, HEAD_DIM), lambda bh, i: (bh // N_HEADS, bh % N_HEADS)),
            pl.BlockSpec((nq, HEAD_DIM, ATT_T), lambda bh, i: (bh // N_HEADS, bh % N_HEADS, 0)),
        ],
        out_specs=pl.BlockSpec((ATT_T, HEAD_DIM), lambda bh, i: ((bh // N_HEADS) * nq + i, bh % N_HEADS)),
        out_shape=jax.ShapeDtypeStruct((batch * seq, D_MODEL), BF16),
        scratch_shapes=[pltpu.VMEM((HEAD_DIM, ATT_T), F32), pltpu.VMEM((SUBLANES, ATT_T), F32),
                        pltpu.VMEM((ATT_T, ATT_T), F32), pltpu.VMEM((ATT_T, ATT_T), F32),
                        pltpu.VMEM((ATT_T, ATT_T), BF16), pltpu.VMEM((ATT_T, ATT_T), BF16)],
        compiler_params=_params("parallel", "arbitrary"),
        name="stick_breaking_attention",
    )(q, kp, vt)


def _proj_kernel(x_ref, a_ref, w_ref, o_ref):
    o_ref[...] = x_ref[...] + jnp.dot(a_ref[...], w_ref[...], preferred_element_type=F32)


def _proj_residual(x, a, w):
    t = x.shape[0]
    return pl.pallas_call(
        _proj_kernel,
        grid=(t // PROJ_ROWS,),
        in_specs=[
            pl.BlockSpec((PROJ_ROWS, D_MODEL), lambda i: (i, 0)),
            pl.BlockSpec((PROJ_ROWS, D_MODEL), lambda i: (i, 0)),
            pl.BlockSpec((D_MODEL, D_MODEL), lambda i: (0, 0)),
        ],
        out_specs=pl.BlockSpec((PROJ_ROWS, D_MODEL), lambda i: (i, 0)),
        out_shape=jax.ShapeDtypeStruct((t, D_MODEL), F32),
        compiler_params=_params("parallel"),
        name="attn_out_proj",
    )(x, a, w)


def _pool_kernel(x_ref, halo_ref, gain_ref, w_ref, scale_ref, o_ref, *level_refs):
    i = pl.program_id(1)
    n_rows = POOL_ROWS + POOL_HALO
    gain = gain_ref[...]

    history = _rms_norm(halo_ref[0], gain)
    level_refs[0][0:POOL_HALO, :] = jnp.where(i > 0, history, 0.0)
    x = x_ref[0]
    level_refs[0][POOL_HALO:, :] = _rms_norm(x, gain)

    n_levels = len(POOL_WINDOWS)
    for lvl in range(1, n_levels + 1):
        shift = 2 ** (lvl - 1)
        start = SUBLANES * lvl
        src, dst = level_refs[lvl - 1], level_refs[lvl]
        src_col0 = max(lvl - 2, 0) * POOL_GROUP
        dst_col0 = (lvl - 1) * POOL_GROUP
        cols = D_MODEL - dst_col0
        off = dst_col0 - src_col0
        rows = n_rows - start
        dst[start:, :] = (src[start:, off:off + cols]
                          + src[pl.ds(start - shift, rows), off:off + cols])

    pos = i * POOL_ROWS + lax.broadcasted_iota(jnp.int32, (POOL_ROWS, 1), 0) + 1
    for g, window in enumerate(POOL_WINDOWS):
        cols = slice(g * POOL_GROUP, (g + 1) * POOL_GROUP)
        wsum = level_refs[g + 1][POOL_HALO:, 0:POOL_GROUP]
        cnt = jnp.minimum(pos, window).astype(F32)
        p = wsum / cnt - level_refs[0][POOL_HALO:, cols]
        y = jnp.dot(p.astype(BF16), w_ref[g], preferred_element_type=F32)
        o_ref[0, :, cols] = x[:, cols] + y * scale_ref[:, cols]


def _pool(x3, gain, w_pool, scale):
    batch, seq, _ = x3.shape
    halo_per_tile = POOL_ROWS // POOL_HALO
    n_rows = POOL_ROWS + POOL_HALO
    n_levels = len(POOL_WINDOWS)
    level_scratch = [pltpu.VMEM((n_rows, D_MODEL), F32)] + [
        pltpu.VMEM((n_rows, D_MODEL - (lvl - 1) * POOL_GROUP), F32) for lvl in range(1, n_levels + 1)]
    return pl.pallas_call(
        _pool_kernel,
        grid=(batch, seq // POOL_ROWS),
        in_specs=[
            pl.BlockSpec((1, POOL_ROWS, D_MODEL), lambda b, i: (b, i, 0)),
            pl.BlockSpec((1, POOL_HALO, D_MODEL),
                         lambda b, i: (b, jnp.maximum(i * halo_per_tile - 1, 0), 0)),
            pl.BlockSpec((1, D_MODEL), lambda b, i: (0, 0)),
            pl.BlockSpec((n_levels, POOL_GROUP, POOL_GROUP), lambda b, i: (0, 0, 0)),
            pl.BlockSpec((1, D_MODEL), lambda b, i: (0, 0)),
        ],
        out_specs=pl.BlockSpec((1, POOL_ROWS, D_MODEL), lambda b, i: (b, i, 0)),
        out_shape=jax.ShapeDtypeStruct(x3.shape, F32),
        scratch_shapes=level_scratch,
        compiler_params=_params("parallel", "arbitrary"),
        name="multiscale_pool",
    )(x3, x3, gain, w_pool, scale)


def kernel(x, norm_gains, ffn1_w_in, ffn1_w_out, ffn2_w_in, ffn2_w_out,
           att_w_qkv, att_w_o, att_qk_gains, pool_w, pool_scale):
    batch, seq, d_model = x.shape
    depth = norm_gains.shape[0]
    xf = x.reshape(batch * seq, d_model)
    for i in range(depth):
        gains = norm_gains[i].reshape(3, 1, d_model)
        xf = _ffn(xf, gains[0], ffn1_w_in[i].astype(BF16), ffn1_w_out[i].astype(BF16))
        j = i // N_MIXERS
        if i % N_MIXERS == 0:
            q, kp, vt = _qkv(xf, gains[1], att_w_qkv[j].astype(BF16), att_qk_gains[j])
            a = _attention(q, kp, vt, batch, seq)
            xf = _proj_residual(xf, a, att_w_o[j].astype(BF16))
        else:
            x3 = _pool(xf.reshape(batch, seq, d_model), gains[1],
                       pool_w[j].astype(BF16), pool_scale[j].reshape(1, d_model))
            xf = x3.reshape(batch * seq, d_model)
        xf = _ffn(xf, gains[2], ffn2_w_in[i].astype(BF16), ffn2_w_out[i].astype(BF16))
    return xf.reshape(batch, seq, d_model)
```
